```python
import jax, jax.numpy as jnp
from jax import lax
import numpy as np

D_MODEL = 1024
BATCH = 16
SEQ = 2048
DEPTH = 4

CHUNK = 64
Q_BLOCK = 2 * CHUNK
CONV_WIDTH = 31
D_CONV = D_MODEL // 2
SB_HEADS = 8
SB_HEAD_DIM = 64
D_SB = SB_HEADS * SB_HEAD_DIM
EPS = 1e-6
IN_SIZES = (2 * D_CONV, D_CONV, D_SB, D_SB, D_SB, D_SB, D_MODEL, D_MODEL)
D_IN = sum(IN_SIZES)
IN_SPLITS = tuple(int(i) for i in np.cumsum(IN_SIZES)[:-1])

kernel_name = "hybrid_conformer_conv_stickbreaking_gated"


def rms_norm(x, g):
    xf = x.astype(jnp.float32)
    y = xf * lax.rsqrt(jnp.mean(xf * xf, axis=-1, keepdims=True) + EPS)
    return (y * g.astype(jnp.float32)).astype(x.dtype)


def layer_norm(x, g, b):
    xf = x.astype(jnp.float32)
    mu = jnp.mean(xf, axis=-1, keepdims=True)
    xc = xf - mu
    var = jnp.mean(xc * xc, axis=-1, keepdims=True)
    y = xc * lax.rsqrt(var + EPS) * g.astype(jnp.float32) + b.astype(jnp.float32)
    return y.astype(x.dtype)


def conformer_conv_branch(glu_in, gate, w_dw, b_dw, ln_g, ln_b, w_proj):
    a, b = jnp.split(glu_in, 2, axis=-1)
    u = a * jax.nn.sigmoid(b)
    u = lax.conv_general_dilated(
        u, w_dw[:, None, :].astype(u.dtype), window_strides=(1,),
        padding=[(CONV_WIDTH - 1, 0)],
        dimension_numbers=("NWC", "WIO", "NWC"),
        feature_group_count=D_CONV) + b_dw
    u = layer_norm(u, ln_g, ln_b)
    u = jax.nn.silu(u) * jax.nn.silu(gate)
    return u @ w_proj


def stick_breaking_attention(q, k, v):
    scale = SB_HEAD_DIM ** -0.5
    seq = q.shape[1]
    outs = []
    for start in range(0, seq, Q_BLOCK):
        end = start + Q_BLOCK
        qb = q[:, start:end]
        kb = k[:, :end]
        vb = v[:, :end]
        z = jnp.einsum("bqhd,bkhd->bhqk", qb, kb,
                       preferred_element_type=jnp.float32) * scale
        t_idx = start + jnp.arange(Q_BLOCK)[:, None]
        s_idx = jnp.arange(end)[None, :]
        mask = s_idx < t_idx
        log_beta = jax.nn.log_sigmoid(z)
        log_1m_beta = jnp.where(mask, log_beta - z, 0.0)
        suffix = lax.cumsum(log_1m_beta, axis=3, reverse=True) - log_1m_beta
        weights = jnp.where(mask, jnp.exp(log_beta + suffix), 0.0)
        outs.append(jnp.einsum("bhqk,bkhd->bqhd", weights.astype(v.dtype), vb))
    return jnp.concatenate(outs, axis=1)


def head_rms_norm(x, g):
    xf = x.astype(jnp.float32)
    y = xf * lax.rsqrt(jnp.mean(xf * xf, axis=-1, keepdims=True) + EPS)
    return (y * g.astype(jnp.float32)).astype(x.dtype)


def setup_inputs(seed: int = 0) -> dict:
    key = jax.random.key(seed)
    ks = jax.random.split(key, 16)
    f32 = jnp.float32
    x = jax.random.normal(ks[0], (BATCH, SEQ, D_MODEL), f32)
    c = jax.random.normal(ks[1], (BATCH, D_MODEL), f32)
    w_ada = jax.random.normal(ks[2], (DEPTH, D_MODEL, 3 * D_MODEL), f32) * (0.1 * D_MODEL ** -0.5)
    b_small = 0.02 * jax.random.normal(ks[3], (DEPTH, 3 * D_MODEL), f32)
    b_ada = b_small + jnp.concatenate(
        [jnp.zeros((DEPTH, 2 * D_MODEL), f32), jnp.ones((DEPTH, D_MODEL), f32)], axis=-1)
    norm_g = 1.0 + 0.02 * jax.random.normal(ks[4], (DEPTH, D_MODEL), f32)
    w_in = jax.random.normal(ks[5], (DEPTH, D_MODEL, D_IN), f32) * D_MODEL ** -0.5
    q_gain = 1.0 + 0.02 * jax.random.normal(ks[6], (DEPTH, SB_HEAD_DIM), f32)
    k_gain = 1.0 + 0.02 * jax.random.normal(ks[7], (DEPTH, SB_HEAD_DIM), f32)
    w_dw = jax.random.normal(ks[8], (DEPTH, CONV_WIDTH, D_CONV), f32) * CONV_WIDTH ** -0.5
    b_dw = 0.02 * jax.random.normal(ks[9], (DEPTH, D_CONV), f32)
    ln_g = 1.0 + 0.02 * jax.random.normal(ks[10], (DEPTH, D_CONV), f32)
    ln_b = 0.02 * jax.random.normal(ks[11], (DEPTH, D_CONV), f32)
    w_conv_out = jax.random.normal(ks[12], (DEPTH, D_CONV, D_MODEL), f32) * D_CONV ** -0.5
    w_sb_out = jax.random.normal(ks[13], (DEPTH, D_SB, D_MODEL), f32) * D_SB ** -0.5
    w_out = jax.random.normal(ks[14], (DEPTH, D_MODEL, D_MODEL), f32) * D_MODEL ** -0.5
    return {"x": x, "c": c, "w_ada": w_ada, "b_ada": b_ada, "norm_g": norm_g, "w_in": w_in,
            "q_gain": q_gain, "k_gain": k_gain, "w_dw": w_dw, "b_dw": b_dw, "ln_g": ln_g,
            "ln_b": ln_b, "w_conv_out": w_conv_out, "w_sb_out": w_sb_out, "w_out": w_out}


def reference(x, c, w_ada, b_ada, norm_g, w_in, q_gain, k_gain, w_dw, b_dw, ln_g, ln_b,
              w_conv_out, w_sb_out, w_out):
    bsz, seq, _ = x.shape
    c_act = jax.nn.silu(c)
    for l in range(DEPTH):
        mod = c_act @ w_ada[l] + b_ada[l]
        shift, scale, gate = jnp.split(mod[:, None, :], 3, axis=-1)
        h = rms_norm(x, norm_g[l]) * (1.0 + scale) + shift
        u = h @ w_in[l]
        glu_in, gate_a, q, k, v, gate_b, m_a, m_b = jnp.split(u, IN_SPLITS, axis=-1)
        y_a = conformer_conv_branch(glu_in, gate_a, w_dw[l], b_dw[l], ln_g[l], ln_b[l], w_conv_out[l])
        q = head_rms_norm(q.reshape(bsz, seq, SB_HEADS, SB_HEAD_DIM), q_gain[l])
        k = head_rms_norm(k.reshape(bsz, seq, SB_HEADS, SB_HEAD_DIM), k_gain[l])
        v = v.reshape(bsz, seq, SB_HEADS, SB_HEAD_DIM)
        o = stick_breaking_attention(q, k, v).reshape(bsz, seq, D_SB)
        y_b = (o * jax.nn.silu(gate_b)) @ w_sb_out[l]
        y = jax.nn.sigmoid(m_a) * y_a + jax.nn.sigmoid(m_b) * y_b
        x = x + gate * (y @ w_out[l])
    return x
```

```python
import functools

import jax
import jax.numpy as jnp
from jax import lax
from jax.experimental import pallas as pl
from jax.experimental.pallas import tpu as pltpu

D_MODEL = 1024
D_CONV = 512
SB_HEADS = 8
SB_HEAD_DIM = 64
D_SB = SB_HEADS * SB_HEAD_DIM
CONV_WIDTH = 31
EPS = 1e-6

C_GLU_A = 0
C_GLU_B = C_GLU_A + D_CONV
C_GATE_A = C_GLU_B + D_CONV
C_Q = C_GATE_A + D_CONV
C_K = C_Q + D_SB
C_V = C_K + D_SB
C_GATE_B = C_V + D_SB
C_M_A = C_GATE_B + D_SB
C_M_B = C_M_A + D_MODEL
D_IN = C_M_B + D_MODEL

LANES = 128
HEAD_PAIRS = D_SB // LANES
SEQ_TILE = 256
CONV_HIST = 32
VMEM_LIMIT_BYTES = 52 * 1024 * 1024

F32 = jnp.float32
BF16 = jnp.bfloat16


def _sigmoid(v):
    return jax.nn.sigmoid(v)


def _silu(v):
    return v * jax.nn.sigmoid(v)


def _ada_kernel(c_ref, w_ref, b_ref, o_ref):
    c = c_ref[...]
    o_ref[0] = jnp.dot(_silu(c), w_ref[0], precision=lax.Precision.HIGHEST,
                       preferred_element_type=F32) + b_ref[0]


def _ada_modulation(c, w_ada, b_ada):
    depth, d, d3 = w_ada.shape
    bsz = c.shape[0]
    n_col = d3 // d
    return pl.pallas_call(
        _ada_kernel,
        grid=(depth, n_col),
        in_specs=[
            pl.BlockSpec((bsz, d), lambda l, j: (0, 0)),
            pl.BlockSpec((1, d, d), lambda l, j: (l, 0, j)),
            pl.BlockSpec((1, 1, d), lambda l, j: (l, 0, j)),
        ],
        out_specs=pl.BlockSpec((1, bsz, d), lambda l, j: (l, 0, j)),
        out_shape=jax.ShapeDtypeStruct((depth, bsz, d3), F32),
        name="ada_modulation",
    )(c, w_ada, b_ada.reshape(depth, 1, d3))


def _sb_block(qh, kt_blk, v_blk, tri, carry, mask):
    z = jnp.dot(qh, kt_blk, preferred_element_type=F32)
    log_beta = jnp.minimum(z, 0.0) - jnp.log(1.0 + jnp.exp(-jnp.abs(z)))
    log_1m = log_beta - z
    if mask is not None:
        log_1m = jnp.where(mask, log_1m, 0.0)
    hi = log_1m.astype(BF16)
    lo = (log_1m - hi.astype(F32)).astype(BF16)
    suffix = (jnp.dot(hi, tri, preferred_element_type=F32)
              + jnp.dot(lo, tri, preferred_element_type=F32))
    w = jnp.exp(log_beta + suffix + carry)
    if mask is not None:
        w = jnp.where(mask, w, 0.0)
    pv = jnp.dot(w.astype(BF16), v_blk, preferred_element_type=F32)
    return pv, carry + jnp.sum(log_1m, axis=-1, keepdims=True)


def _layer_kernel(x_ref, mod_ref, ng_ref, win_ref, qg_ref, kg_ref, wdw_ref, bdw_ref, lng_ref, lnb_ref,
                  wca_ref, wsb_ref, wout_ref, hsum_ref, tri_ref,
                  o_ref,
                  kt_s, ve_s, vo_s, gh_s, att_s):
    i = pl.program_id(1)
    t = x_ref.shape[1]
    d = D_MODEL

    x = x_ref[0]
    mod = mod_ref[0]
    shift = mod[:, 0:d]
    scale = mod[:, d:2 * d]
    gate = mod[:, 2 * d:3 * d]

    rs = lax.rsqrt(jnp.mean(x * x, axis=-1, keepdims=True) + EPS)
    h = (x * rs * ng_ref[...]) * (1.0 + scale) + shift
    hb = h.astype(BF16)

    def proj(lo, width):
        return jnp.dot(hb, win_ref[:, lo:lo + width], preferred_element_type=F32)

    @pl.when(i == 0)
    def _():
        gh_s[0:CONV_HIST, :] = jnp.zeros((CONV_HIST, D_CONV), F32)

    gh_s[CONV_HIST:CONV_HIST + t, :] = proj(C_GLU_A, D_CONV) * _sigmoid(proj(C_GLU_B, D_CONV))
    base = CONV_HIST - (CONV_WIDTH - 1)
    conv = jnp.broadcast_to(bdw_ref[...], (t, D_CONV))
    for k in range(CONV_WIDTH):
        conv = conv + wdw_ref[k:k + 1, :] * gh_s[base + k:base + k + t, :]
    gh_s[0:CONV_HIST, :] = gh_s[t:t + CONV_HIST, :]

    mu = jnp.mean(conv, axis=-1, keepdims=True)
    xc = conv - mu
    var = jnp.mean(xc * xc, axis=-1, keepdims=True)
    ln = xc * lax.rsqrt(var + EPS) * lng_ref[...] + lnb_ref[...]
    ca = _silu(ln) * _silu(proj(C_GATE_A, D_CONV))
    y_a = jnp.dot(ca.astype(BF16), wca_ref[...], preferred_element_type=F32)

    lane = lax.broadcasted_iota(jnp.int32, (t, D_SB), 1)
    even = (lane & SB_HEAD_DIM) == 0

    def head_norm(v, gain):
        ms = jnp.dot((v * v).astype(BF16), hsum_ref[...], preferred_element_type=F32)
        return v * lax.rsqrt(ms + EPS) * gain

    qn = head_norm(proj(C_Q, D_SB), qg_ref[...]) * (SB_HEAD_DIM ** -0.5)
    q_even = jnp.where(even, qn, 0.0).astype(BF16)
    q_odd = jnp.where(even, 0.0, qn).astype(BF16)

    kn = head_norm(proj(C_K, D_SB), kg_ref[...])
    col0 = pl.multiple_of(i * t, t)
    kt_s[:, pl.ds(col0, t)] = kn.T.astype(BF16)

    v = proj(C_V, D_SB)
    ve_s[pl.ds(col0, t), :] = jnp.where(even, v, 0.0).astype(BF16)
    vo_s[pl.ds(col0, t), :] = jnp.where(even, 0.0, v).astype(BF16)

    tri = tri_ref[...]
    row = lax.broadcasted_iota(jnp.int32, (t, t), 0)
    colk = lax.broadcasted_iota(jnp.int32, (t, t), 1)
    causal = colk < row

    for p in range(HEAD_PAIRS):
        lanes = slice(p * LANES, (p + 1) * LANES)
        qe_p = q_even[:, lanes]
        qo_p = q_odd[:, lanes]

        def sweep(j, carry_e, carry_o, acc, mask, lanes=lanes, qe_p=qe_p, qo_p=qo_p):
            k0 = pl.multiple_of(j * t, t)
            kt_blk = kt_s[lanes, pl.ds(k0, t)]
            pv_e, carry_e = _sb_block(qe_p, kt_blk, ve_s[pl.ds(k0, t), lanes], tri, carry_e, mask)
            pv_o, carry_o = _sb_block(qo_p, kt_blk, vo_s[pl.ds(k0, t), lanes], tri, carry_o, mask)
            return carry_e, carry_o, acc + pv_e + pv_o

        zero_c = jnp.zeros((t, 1), F32)
        state = sweep(i, zero_c, zero_c, jnp.zeros((t, LANES), F32), causal)

        def body(it, st, sweep=sweep):
            return sweep(i - 1 - it, st[0], st[1], st[2], None)

        state = lax.fori_loop(0, i, body, state)
        att_s[:, lanes] = state[2]

    ob = att_s[...] * _silu(proj(C_GATE_B, D_SB))
    y_b = jnp.dot(ob.astype(BF16), wsb_ref[...], preferred_element_type=F32)

    y = _sigmoid(proj(C_M_A, D_MODEL)) * y_a + _sigmoid(proj(C_M_B, D_MODEL)) * y_b
    o_ref[0] = x + gate * jnp.dot(y.astype(BF16), wout_ref[...], preferred_element_type=F32)


def _resident(shape):
    zeros = (0,) * len(shape)
    return pl.BlockSpec(shape, lambda b, i: zeros, pipeline_mode=pl.Buffered(1))


def _layer(x, mod, ng, win, qg, kg, wdw, bdw, lng, lnb, wca, wsb, wout, hsum, tri):
    bsz, seq, d = x.shape
    t = SEQ_TILE
    n_tiles = seq // t
    consts = (ng, win, qg, kg, wdw, bdw, lng, lnb, wca, wsb, wout, hsum, tri)
    return pl.pallas_call(
        _layer_kernel,
        grid=(bsz, n_tiles),
        in_specs=[
            pl.BlockSpec((1, t, d), lambda b, i: (b, i, 0)),
            pl.BlockSpec((1, 1, 3 * d), lambda b, i: (b, 0, 0)),
        ] + [_resident(a.shape) for a in consts],
        out_specs=pl.BlockSpec((1, t, d), lambda b, i: (b, i, 0)),
        out_shape=jax.ShapeDtypeStruct(x.shape, x.dtype),
        scratch_shapes=[
            pltpu.VMEM((D_SB, seq), BF16),
            pltpu.VMEM((seq, D_SB), BF16),
            pltpu.VMEM((seq, D_SB), BF16),
            pltpu.VMEM((CONV_HIST + t, D_CONV), F32),
            pltpu.VMEM((t, D_SB), F32),
        ],
        compiler_params=pltpu.CompilerParams(
            dimension_semantics=("arbitrary", "arbitrary"),
            vmem_limit_bytes=VMEM_LIMIT_BYTES),
        name="mixer_layer",
    )(x, mod, *consts)


def kernel(x, c, w_ada, b_ada, norm_g, w_in, q_gain, k_gain, w_dw, b_dw, ln_g, ln_b,
           w_conv_out, w_sb_out, w_out):
    depth = w_in.shape[0]
    bsz = x.shape[0]
    mod = _ada_modulation(c, w_ada, b_ada)

    head = jnp.arange(D_SB, dtype=jnp.int32) // SB_HEAD_DIM
    hsum = jnp.where(head[:, None] == head[None, :], 1.0 / SB_HEAD_DIM, 0.0).astype(BF16)
    idx = jnp.arange(SEQ_TILE, dtype=jnp.int32)
    tri = (idx[:, None] > idx[None, :]).astype(BF16)

    for l in range(depth):
        x = _layer(
            x, mod[l].reshape(bsz, 1, 3 * D_MODEL), norm_g[l].reshape(1, D_MODEL),
            w_in[l].astype(BF16),
            jnp.tile(q_gain[l], SB_HEADS).reshape(1, D_SB), jnp.tile(k_gain[l], SB_HEADS).reshape(1, D_SB),
            w_dw[l], b_dw[l].reshape(1, D_CONV), ln_g[l].reshape(1, D_CONV), ln_b[l].reshape(1, D_CONV),
            w_conv_out[l].astype(BF16), w_sb_out[l].astype(BF16), w_out[l].astype(BF16),
            hsum, tri)
    return x
```

```python
import jax
import jax.numpy as jnp
from jax import lax
from jax.experimental import pallas as pl
from jax.experimental.pallas import tpu as pltpu

D_MODEL = 1024
D_CONV = 512
SB_HEADS = 8
SB_HEAD_DIM = 64
D_SB = SB_HEADS * SB_HEAD_DIM
CONV_WIDTH = 31
EPS = 1e-6

C_GLU_A = 0
C_GLU_B = C_GLU_A + D_CONV
C_GATE_A = C_GLU_B + D_CONV
C_Q = C_GATE_A + D_CONV
C_K = C_Q + D_SB
C_V = C_K + D_SB
C_GATE_B = C_V + D_SB
C_M_A = C_GATE_B + D_SB
C_M_B = C_M_A + D_MODEL
D_IN = C_M_B + D_MODEL

LANES = 128
SUBLANES = 8
HEAD_PAIRS = D_SB // LANES
SEQ_TILE = 256
CONV_HIST = 32
SHIFT_ROWS = SEQ_TILE + CONV_HIST - SUBLANES
VMEM_LIMIT_BYTES = 52 * 1024 * 1024

F32 = jnp.float32
BF16 = jnp.bfloat16


def _sigmoid(v):
    return jax.nn.sigmoid(v)


def _silu(v):
    return v * jax.nn.sigmoid(v)


def _ada_kernel(c_ref, w_ref, b_ref, o_ref):
    c = c_ref[...]
    o_ref[0] = jnp.dot(_silu(c), w_ref[0], precision=lax.Precision.HIGHEST,
                       preferred_element_type=F32) + b_ref[0]


def _ada_modulation(c, w_ada, b_ada):
    depth, d, d3 = w_ada.shape
    bsz = c.shape[0]
    n_col = d3 // d
    return pl.pallas_call(
        _ada_kernel,
        grid=(depth, n_col),
        in_specs=[
            pl.BlockSpec((bsz, d), lambda l, j: (0, 0)),
            pl.BlockSpec((1, d, d), lambda l, j: (l, 0, j)),
            pl.BlockSpec((1, 1, d), lambda l, j: (l, 0, j)),
        ],
        out_specs=pl.BlockSpec((1, bsz, d), lambda l, j: (l, 0, j)),
        out_shape=jax.ShapeDtypeStruct((depth, bsz, d3), F32),
        name="ada_modulation",
    )(c, w_ada, b_ada.reshape(depth, 1, d3))


def _sb_scores(z, tri, mask):
    log_beta = jnp.minimum(z, 0.0) - jnp.log(1.0 + jnp.exp(-jnp.abs(z)))
    log_1m = log_beta - z
    if mask is not None:
        log_1m = jnp.where(mask, log_1m, 0.0)
    suffix = jnp.dot(log_1m.astype(BF16), tri, preferred_element_type=F32)
    return log_beta, suffix, jnp.sum(log_1m, axis=-1, keepdims=True)


def _sb_weights(log_beta, suffix, mask):
    w = jnp.exp(log_beta + suffix)
    if mask is not None:
        w = jnp.where(mask, w, 0.0)
    return w.astype(BF16)


def _layer_kernel(x_ref, mod_ref, ng_ref, win_ref, qg_ref, kg_ref, wdw_ref, bdw_ref, lng_ref, lnb_ref,
                  wca_ref, wsb_ref, wout_ref, hsum_ref, tri_ref,
                  o_ref,
                  kt_s, ve_s, vo_s, qe_s, qo_s, gh_s, sh_s, att_s):
    i = pl.program_id(1)
    t = x_ref.shape[1]
    d = D_MODEL

    x = x_ref[0]
    mod = mod_ref[0]
    shift = mod[:, 0:d]
    scale = mod[:, d:2 * d]
    gate = mod[:, 2 * d:3 * d]

    rs = lax.rsqrt(jnp.mean(x * x, axis=-1, keepdims=True) + EPS)
    h = (x * rs * ng_ref[...]) * (1.0 + scale) + shift
    hb = h.astype(BF16)

    def proj(lo, width):
        return jnp.dot(hb, win_ref[:, lo:lo + width], preferred_element_type=F32)

    @pl.when(i == 0)
    def _():
        gh_s[0:CONV_HIST, :] = jnp.zeros((CONV_HIST, D_CONV), F32)

    gh_s[CONV_HIST:CONV_HIST + t, :] = proj(C_GLU_A, D_CONV) * _sigmoid(proj(C_GLU_B, D_CONV))
    for r in range(1, SUBLANES):
        sh_s[r - 1] = gh_s[r:r + SHIFT_ROWS, :]

    lane = lax.broadcasted_iota(jnp.int32, (t, D_SB), 1)
    even = (lane & SB_HEAD_DIM) == 0
    col0 = pl.multiple_of(i * t, t)
    rows_now = pl.ds(col0, t)

    def head_norm(v, gain):
        ms = jnp.dot((v * v).astype(BF16), hsum_ref[...], preferred_element_type=F32)
        return v * lax.rsqrt(ms + EPS) * gain

    def do_q():
        qn = head_norm(proj(C_Q, D_SB), qg_ref[...]) * (SB_HEAD_DIM ** -0.5)
        qe_s[...] = jnp.where(even, qn, 0.0).astype(BF16)
        qo_s[...] = jnp.where(even, 0.0, qn).astype(BF16)

    def do_k():
        kt_s[:, rows_now] = head_norm(proj(C_K, D_SB), kg_ref[...]).T.astype(BF16)

    def do_v():
        v = proj(C_V, D_SB)
        ve_s[rows_now, :] = jnp.where(even, v, 0.0).astype(BF16)
        vo_s[rows_now, :] = jnp.where(even, 0.0, v).astype(BF16)

    side = {}
    mxu_work = [
        do_q, do_k, do_v,
        lambda: side.update(gate_a=_silu(proj(C_GATE_A, D_CONV))),
        lambda: side.update(gate_b=_silu(proj(C_GATE_B, D_SB))),
    ]
    taps_per_task = -(-CONV_WIDTH // len(mxu_work))
    base = CONV_HIST - (CONV_WIDTH - 1)
    conv = jnp.broadcast_to(bdw_ref[...], (t, D_CONV))
    for k in range(CONV_WIDTH):
        if k % taps_per_task == 0:
            mxu_work[k // taps_per_task]()
        r, a = (base + k) % SUBLANES, (base + k) // SUBLANES * SUBLANES
        rows = gh_s[a:a + t, :] if r == 0 else sh_s[r - 1, a:a + t, :]
        conv = conv + wdw_ref[k:k + 1, :] * rows
    gh_s[0:CONV_HIST, :] = gh_s[t:t + CONV_HIST, :]

    mu = jnp.mean(conv, axis=-1, keepdims=True)
    xc = conv - mu
    var = jnp.mean(xc * xc, axis=-1, keepdims=True)
    ln = xc * lax.rsqrt(var + EPS) * lng_ref[...] + lnb_ref[...]
    ca = (_silu(ln) * side["gate_a"]).astype(BF16)

    def sweep(j, carries, mask, fillers=()):
        fillers = list(fillers)
        k0 = pl.multiple_of(j * t, t)
        keys = pl.ds(k0, t)
        lanes = [slice(h // 2 * LANES, (h // 2 + 1) * LANES) for h in range(SB_HEADS)]
        q_s = [qe_s, qo_s]
        v_s = [ve_s, vo_s]
        z = [jnp.dot(q_s[h % 2][:, lanes[h]], kt_s[lanes[h], keys], preferred_element_type=F32)
             for h in range(SB_HEADS)]
        terms = []
        for h in range(SB_HEADS):
            if fillers and h % 2 == 0:
                fillers.pop(0)()
            terms.append(_sb_scores(z[h], tri_ref[...], mask))
        pv = [jnp.dot(_sb_weights(terms[h][0], terms[h][1], mask), v_s[h % 2][keys, lanes[h]],
                      preferred_element_type=F32) for h in range(SB_HEADS)]
        for p in range(HEAD_PAIRS):
            e, o = 2 * p, 2 * p + 1
            if mask is not None:
                att_s[:, lanes[e]] = pv[e] + pv[o]
            else:
                att_s[:, lanes[e]] += pv[e] * jnp.exp(carries[e]) + pv[o] * jnp.exp(carries[o])
        return tuple(carries[h] + terms[h][2] for h in range(SB_HEADS))

    row = lax.broadcasted_iota(jnp.int32, (t, t), 0)
    colk = lax.broadcasted_iota(jnp.int32, (t, t), 1)
    zero_c = jnp.zeros((t, 1), F32)
    diag_fillers = [
        lambda: side.update(y_a=jnp.dot(ca, wca_ref[...], preferred_element_type=F32)),
        lambda: side.update(y_a=_sigmoid(proj(C_M_A, D_MODEL)) * side["y_a"]),
        lambda: side.update(m_b=_sigmoid(proj(C_M_B, D_MODEL))),
    ]
    carries = sweep(i, (zero_c,) * SB_HEADS, colk < row, diag_fillers)

    def live(cs):
        top = cs[0]
        for c in cs[1:]:
            top = jnp.maximum(top, c)
        return (jnp.max(jnp.exp(top)) > 0.0).astype(jnp.int32)

    def earlier_block(st):
        cs = sweep(i - 1 - st[0], st[2:], None)
        return (st[0] + 1, live(cs)) + cs

    lax.while_loop(lambda st: jnp.logical_and(st[0] < i, st[1] > 0), earlier_block,
                   (jnp.int32(0), jnp.int32(1)) + carries)

    ob = att_s[...] * side["gate_b"]
    y_b = jnp.dot(ob.astype(BF16), wsb_ref[...], preferred_element_type=F32)

    y = side["y_a"] + side["m_b"] * y_b
    o_ref[0] = x + gate * jnp.dot(y.astype(BF16), wout_ref[...], preferred_element_type=F32)


def _resident(shape):
    zeros = (0,) * len(shape)
    return pl.BlockSpec(shape, lambda b, i: zeros, pipeline_mode=pl.Buffered(1))


def _layer(x, mod, ng, win, qg, kg, wdw, bdw, lng, lnb, wca, wsb, wout, hsum, tri):
    bsz, seq, d = x.shape
    t = SEQ_TILE
    n_tiles = seq // t
    consts = (ng, win, qg, kg, wdw, bdw, lng, lnb, wca, wsb, wout, hsum, tri)
    return pl.pallas_call(
        _layer_kernel,
        grid=(bsz, n_tiles),
        in_specs=[
            pl.BlockSpec((1, t, d), lambda b, i: (b, i, 0)),
            pl.BlockSpec((1, 1, 3 * d), lambda b, i: (b, 0, 0)),
        ] + [_resident(a.shape) for a in consts],
        out_specs=pl.BlockSpec((1, t, d), lambda b, i: (b, i, 0)),
        out_shape=jax.ShapeDtypeStruct(x.shape, x.dtype),
        scratch_shapes=[
            pltpu.VMEM((D_SB, seq), BF16),
            pltpu.VMEM((seq, D_SB), BF16),
            pltpu.VMEM((seq, D_SB), BF16),
            pltpu.VMEM((t, D_SB), BF16),
            pltpu.VMEM((t, D_SB), BF16),
            pltpu.VMEM((CONV_HIST + t, D_CONV), F32),
            pltpu.VMEM((SUBLANES - 1, SHIFT_ROWS, D_CONV), F32),
            pltpu.VMEM((t, D_SB), F32),
        ],
        compiler_params=pltpu.CompilerParams(
            dimension_semantics=("arbitrary", "arbitrary"),
            vmem_limit_bytes=VMEM_LIMIT_BYTES),
        name="mixer_layer",
    )(x, mod, *consts)


def kernel(x, c, w_ada, b_ada, norm_g, w_in, q_gain, k_gain, w_dw, b_dw, ln_g, ln_b,
           w_conv_out, w_sb_out, w_out):
    depth = w_in.shape[0]
    bsz = x.shape[0]
    mod = _ada_modulation(c, w_ada, b_ada)

    head = jnp.arange(D_SB, dtype=jnp.int32) // SB_HEAD_DIM
    hsum = jnp.where(head[:, None] == head[None, :], 1.0 / SB_HEAD_DIM, 0.0).astype(BF16)
    idx = jnp.arange(SEQ_TILE, dtype=jnp.int32)
    tri = (idx[:, None] > idx[None, :]).astype(BF16)

    for l in range(depth):
        x = _layer(
            x, mod[l].reshape(bsz, 1, 3 * D_MODEL), norm_g[l].reshape(1, D_MODEL),
            w_in[l].astype(BF16),
            jnp.tile(q_gain[l], SB_HEADS).reshape(1, D_SB), jnp.tile(k_gain[l], SB_HEADS).reshape(1, D_SB),
            w_dw[l], b_dw[l].reshape(1, D_CONV), ln_g[l].reshape(1, D_CONV), ln_b[l].reshape(1, D_CONV),
            w_conv_out[l].astype(BF16), w_sb_out[l].astype(BF16), w_out[l].astype(BF16),
            hsum, tri)
    return x
```

```python
import jax
import jax.numpy as jnp
from jax import lax
from jax.experimental import pallas as pl
from jax.experimental.pallas import tpu as pltpu

D_MODEL = 1024
D_CONV = 512
SB_HEADS = 8
SB_HEAD_DIM = 64
D_SB = SB_HEADS * SB_HEAD_DIM
CONV_WIDTH = 31
EPS = 1e-6

C_GLU_A = 0
C_GLU_B = C_GLU_A + D_CONV
C_GATE_A = C_GLU_B + D_CONV
C_Q = C_GATE_A + D_CONV
C_K = C_Q + D_SB
C_V = C_K + D_SB
C_GATE_B = C_V + D_SB
C_M_A = C_GATE_B + D_SB
C_M_B = C_M_A + D_MODEL
D_IN = C_M_B + D_MODEL

LANES = 128
SUBLANES = 8
HEAD_PAIRS = D_SB // LANES
SEQ_TILE = 256
STAGE_SKEW = 1
CONV_HIST = 32
SHIFT_ROWS = SEQ_TILE + CONV_HIST - SUBLANES
VMEM_LIMIT_BYTES = 52 * 1024 * 1024

F32 = jnp.float32
BF16 = jnp.bfloat16


def _sigmoid(v):
    return jax.nn.sigmoid(v)


def _silu(v):
    return v * jax.nn.sigmoid(v)


def _ada_kernel(c_ref, w_ref, b_ref, o_ref):
    c = c_ref[...]
    o_ref[0] = jnp.dot(_silu(c), w_ref[0], precision=lax.Precision.HIGHEST,
                       preferred_element_type=F32) + b_ref[0]


def _ada_modulation(c, w_ada, b_ada):
    depth, d, d3 = w_ada.shape
    bsz = c.shape[0]
    n_col = d3 // d
    return pl.pallas_call(
        _ada_kernel,
        grid=(depth, n_col),
        in_specs=[
            pl.BlockSpec((bsz, d), lambda l, j: (0, 0)),
            pl.BlockSpec((1, d, d), lambda l, j: (l, 0, j)),
            pl.BlockSpec((1, 1, d), lambda l, j: (l, 0, j)),
        ],
        out_specs=pl.BlockSpec((1, bsz, d), lambda l, j: (l, 0, j)),
        out_shape=jax.ShapeDtypeStruct((depth, bsz, d3), F32),
        name="ada_modulation",
    )(c, w_ada, b_ada.reshape(depth, 1, d3))


def _sb_scores(z, tri, mask):
    neg_z = -z
    log_1m = jnp.minimum(neg_z, 0.0) - jnp.log(1.0 + jnp.exp(jnp.minimum(z, neg_z)))
    if mask is not None:
        log_1m = jnp.where(mask, log_1m, 0.0)
    suffix = jnp.dot(log_1m.astype(BF16), tri, preferred_element_type=F32)
    return suffix, jnp.sum(log_1m, axis=-1, keepdims=True)


def _sb_weights(z, suffix, mask):
    w = jnp.exp(z + suffix)
    if mask is not None:
        w = jnp.where(mask, w, 0.0)
    return w.astype(BF16)


def _layer_kernel(x_ref, mod_ref, ng_ref, win_ref, qg_ref, kg_ref, wdw_ref, bdw_ref, lng_ref, lnb_ref,
                  wca_ref, wsb_ref, wout_ref, hsum_ref, tri_ref,
                  o_ref,
                  kt_s, ve_s, vo_s, qe_s, qo_s, gh_s, sh_s, att_s):
    i = pl.program_id(1)
    t = x_ref.shape[1]
    d = D_MODEL

    x = x_ref[0]
    mod = mod_ref[0]
    shift = mod[:, 0:d]
    scale = mod[:, d:2 * d]
    gate = mod[:, 2 * d:3 * d]

    rs = lax.rsqrt(jnp.mean(x * x, axis=-1, keepdims=True) + EPS)
    h = (x * rs * ng_ref[...]) * (1.0 + scale) + shift
    hb = h.astype(BF16)

    def proj(lo, width):
        return jnp.dot(hb, win_ref[:, lo:lo + width], preferred_element_type=F32)

    @pl.when(i == 0)
    def _():
        gh_s[0:CONV_HIST, :] = jnp.zeros((CONV_HIST, D_CONV), F32)

    gh_s[CONV_HIST:CONV_HIST + t, :] = proj(C_GLU_A, D_CONV) * _sigmoid(proj(C_GLU_B, D_CONV))

    lane = lax.broadcasted_iota(jnp.int32, (t, D_SB), 1)
    even = (lane & SB_HEAD_DIM) == 0
    col0 = pl.multiple_of(i * t, t)
    rows_now = pl.ds(col0, t)
    half = D_MODEL // 2

    def head_norm(v, gain):
        ms = jnp.dot((v * v).astype(BF16), hsum_ref[...], preferred_element_type=F32)
        return v * lax.rsqrt(ms + EPS) * gain

    def do_q():
        qn = head_norm(proj(C_Q, D_SB), qg_ref[...]) * (SB_HEAD_DIM ** -0.5)
        qe_s[...] = jnp.where(even, qn, 0.0).astype(BF16)
        qo_s[...] = jnp.where(even, 0.0, qn).astype(BF16)

    def do_k():
        kt_s[:, rows_now] = head_norm(proj(C_K, D_SB), kg_ref[...]).T.astype(BF16)

    def do_v():
        v = proj(C_V, D_SB)
        ve_s[rows_now, :] = jnp.where(even, v, 0.0).astype(BF16)
        vo_s[rows_now, :] = jnp.where(even, 0.0, v).astype(BF16)

    side = {}
    mxu_work = [
        do_q, do_k, do_v,
        lambda: side.update(gate_a=_silu(proj(C_GATE_A, D_CONV))),
        lambda: side.update(gate_b=_silu(proj(C_GATE_B, D_SB))),
    ]
    base = CONV_HIST - (CONV_WIDTH - 1)
    conv = jnp.broadcast_to(bdw_ref[...], (t, D_CONV))
    for r in range(SUBLANES):
        if r < len(mxu_work):
            mxu_work[r]()
        if r > 0:
            sh_s[r - 1] = gh_s[r:r + SHIFT_ROWS, :]
        for k in range(CONV_WIDTH):
            if (base + k) % SUBLANES == r:
                a = (base + k) // SUBLANES * SUBLANES
                rows = gh_s[a:a + t, :] if r == 0 else sh_s[r - 1, a:a + t, :]
                conv = conv + wdw_ref[k:k + 1, :] * rows
    gh_s[0:CONV_HIST, :] = gh_s[t:t + CONV_HIST, :]

    mu = jnp.mean(conv, axis=-1, keepdims=True)
    xc = conv - mu
    var = jnp.mean(xc * xc, axis=-1, keepdims=True)
    ln = xc * lax.rsqrt(var + EPS) * lng_ref[...] + lnb_ref[...]
    ca = (_silu(ln) * side["gate_a"]).astype(BF16)

    merge_gates = [
        lambda: side.update(m_a0=_sigmoid(proj(C_M_A, half))),
        lambda: side.update(m_a1=_sigmoid(proj(C_M_A + half, half))),
        lambda: side.update(m_b0=_sigmoid(proj(C_M_B, half))),
        lambda: side.update(m_b1=_sigmoid(proj(C_M_B + half, half))),
    ]

    def sweep(j, carries, mask, fillers=()):
        fillers = list(fillers)
        k0 = pl.multiple_of(j * t, t)
        keys = pl.ds(k0, t)
        lanes = [slice(h // 2 * LANES, (h // 2 + 1) * LANES) for h in range(SB_HEADS)]
        q_s = [qe_s, qo_s]
        v_s = [ve_s, vo_s]
        z, terms, pv = [], [], []
        for step in range(SB_HEADS + 2 * STAGE_SKEW):
            if step < SB_HEADS:
                h = step
                z.append(jnp.dot(q_s[h % 2][:, lanes[h]], kt_s[lanes[h], keys], preferred_element_type=F32))
            if STAGE_SKEW <= step < SB_HEADS + STAGE_SKEW:
                h = step - STAGE_SKEW
                if fillers and h % 2 == 0:
                    fillers.pop(0)()
                terms.append(_sb_scores(z[h], tri_ref[...], mask))
            if step >= 2 * STAGE_SKEW:
                h = step - 2 * STAGE_SKEW
                pv.append(jnp.dot(_sb_weights(z[h], terms[h][0], mask), v_s[h % 2][keys, lanes[h]],
                                  preferred_element_type=F32))
        for p in range(HEAD_PAIRS):
            e, o = 2 * p, 2 * p + 1
            if mask is not None:
                att_s[:, lanes[e]] = pv[e] + pv[o]
            else:
                att_s[:, lanes[e]] += pv[e] * jnp.exp(carries[e]) + pv[o] * jnp.exp(carries[o])
        return tuple(carries[h] + terms[h][1] for h in range(SB_HEADS))

    row = lax.broadcasted_iota(jnp.int32, (t, t), 0)
    colk = lax.broadcasted_iota(jnp.int32, (t, t), 1)
    zero_c = jnp.zeros((t, 1), F32)
    carries = sweep(i, (zero_c,) * SB_HEADS, colk < row, merge_gates)
    y_a = jnp.dot(ca, wca_ref[...], preferred_element_type=F32)
    y_a = jnp.concatenate([side["m_a0"] * y_a[:, :half], side["m_a1"] * y_a[:, half:]], axis=1)

    def live(cs):
        top = cs[0]
        for c in cs[1:]:
            top = jnp.maximum(top, c)
        return (jnp.max(jnp.exp(top)) > 0.0).astype(jnp.int32)

    def earlier_block(st):
        cs = sweep(i - 1 - st[0], st[2:], None)
        return (st[0] + 1, live(cs)) + cs

    lax.while_loop(lambda st: jnp.logical_and(st[0] < i, st[1] > 0), earlier_block,
                   (jnp.int32(0), jnp.int32(1)) + carries)

    ob = att_s[...] * side["gate_b"]
    y_b = jnp.dot(ob.astype(BF16), wsb_ref[...], preferred_element_type=F32)

    y = y_a + jnp.concatenate([side["m_b0"] * y_b[:, :half], side["m_b1"] * y_b[:, half:]], axis=1)
    o_ref[0] = x + gate * jnp.dot(y.astype(BF16), wout_ref[...], preferred_element_type=F32)


def _resident(shape):
    zeros = (0,) * len(shape)
    return pl.BlockSpec(shape, lambda b, i: zeros, pipeline_mode=pl.Buffered(1))


def _layer(x, mod, ng, win, qg, kg, wdw, bdw, lng, lnb, wca, wsb, wout, hsum, tri):
    bsz, seq, d = x.shape
    t = SEQ_TILE
    n_tiles = seq // t
    consts = (ng, win, qg, kg, wdw, bdw, lng, lnb, wca, wsb, wout, hsum, tri)
    return pl.pallas_call(
        _layer_kernel,
        grid=(bsz, n_tiles),
        in_specs=[
            pl.BlockSpec((1, t, d), lambda b, i: (b, i, 0)),
            pl.BlockSpec((1, 1, 3 * d), lambda b, i: (b, 0, 0)),
        ] + [_resident(a.shape) for a in consts],
        out_specs=pl.BlockSpec((1, t, d), lambda b, i: (b, i, 0)),
        out_shape=jax.ShapeDtypeStruct(x.shape, x.dtype),
        scratch_shapes=[
            pltpu.VMEM((D_SB, seq), BF16),
            pltpu.VMEM((seq, D_SB), BF16),
            pltpu.VMEM((seq, D_SB), BF16),
            pltpu.VMEM((t, D_SB), BF16),
            pltpu.VMEM((t, D_SB), BF16),
            pltpu.VMEM((CONV_HIST + t, D_CONV), F32),
            pltpu.VMEM((SUBLANES - 1, SHIFT_ROWS, D_CONV), F32),
            pltpu.VMEM((t, D_SB), F32),
        ],
        compiler_params=pltpu.CompilerParams(
            dimension_semantics=("arbitrary", "arbitrary"),
            vmem_limit_bytes=VMEM_LIMIT_BYTES),
        name="mixer_layer",
    )(x, mod, *consts)


def kernel(x, c, w_ada, b_ada, norm_g, w_in, q_gain, k_gain, w_dw, b_dw, ln_g, ln_b,
           w_conv_out, w_sb_out, w_out):
    depth = w_in.shape[0]
    bsz = x.shape[0]
    mod = _ada_modulation(c, w_ada, b_ada)

    head = jnp.arange(D_SB, dtype=jnp.int32) // SB_HEAD_DIM
    hsum = jnp.where(head[:, None] == head[None, :], 1.0 / SB_HEAD_DIM, 0.0).astype(BF16)
    idx = jnp.arange(SEQ_TILE, dtype=jnp.int32)
    tri = (idx[:, None] >= idx[None, :]).astype(BF16)

    for l in range(depth):
        x = _layer(
            x, mod[l].reshape(bsz, 1, 3 * D_MODEL), norm_g[l].reshape(1, D_MODEL),
            w_in[l].astype(BF16),
            jnp.tile(q_gain[l], SB_HEADS).reshape(1, D_SB), jnp.tile(k_gain[l], SB_HEADS).reshape(1, D_SB),
            w_dw[l], b_dw[l].reshape(1, D_CONV), ln_g[l].reshape(1, D_CONV), ln_b[l].reshape(1, D_CONV),
            w_conv_out[l].astype(BF16), w_sb_out[l].astype(BF16), w_out[l].astype(BF16),
            hsum, tri)
    return x
```

```python
import jax
import jax.numpy as jnp
from jax import lax
from jax.experimental import pallas as pl
from jax.experimental.pallas import tpu as pltpu

D_MODEL = 1024
D_CONV = 512
SB_HEADS = 8
SB_HEAD_DIM = 64
D_SB = SB_HEADS * SB_HEAD_DIM
CONV_WIDTH = 31
EPS = 1e-6

C_GLU_A = 0
C_GLU_B = C_GLU_A + D_CONV
C_GATE_A = C_GLU_B + D_CONV
C_Q = C_GATE_A + D_CONV
C_K = C_Q + D_SB
C_V = C_K + D_SB
C_GATE_B = C_V + D_SB
C_M_A = C_GATE_B + D_SB
C_M_B = C_M_A + D_MODEL
D_IN = C_M_B + D_MODEL

LANES = 128
SUBLANES = 8
HEAD_PAIRS = D_SB // LANES
SEQ_TILE = 512
KEY_BLOCK = 256
TILE_PASS_SKEW = 2
LOOP_PASS_SKEW = 1
CONV_HIST = 32
SHIFT_ROWS = SEQ_TILE + CONV_HIST - SUBLANES
VMEM_LIMIT_BYTES = 58 * 1024 * 1024

F32 = jnp.float32
BF16 = jnp.bfloat16


def _sigmoid(v):
    return jax.nn.sigmoid(v)


def _silu(v):
    return v * jax.nn.sigmoid(v)


def _ada_kernel(c_ref, w_ref, b_ref, o_ref):
    c = c_ref[...]
    o_ref[0] = jnp.dot(_silu(c), w_ref[0], precision=lax.Precision.HIGHEST,
                       preferred_element_type=F32) + b_ref[0]


def _ada_modulation(c, w_ada, b_ada):
    depth, d, d3 = w_ada.shape
    bsz = c.shape[0]
    n_col = d3 // d
    return pl.pallas_call(
        _ada_kernel,
        grid=(depth, n_col),
        in_specs=[
            pl.BlockSpec((bsz, d), lambda l, j: (0, 0)),
            pl.BlockSpec((1, d, d), lambda l, j: (l, 0, j)),
            pl.BlockSpec((1, 1, d), lambda l, j: (l, 0, j)),
        ],
        out_specs=pl.BlockSpec((1, bsz, d), lambda l, j: (l, 0, j)),
        out_shape=jax.ShapeDtypeStruct((depth, bsz, d3), F32),
        name="ada_modulation",
    )(c, w_ada, b_ada.reshape(depth, 1, d3))


def _sb_scores(z, tri, mask):
    neg_z = -z
    log_1m = jnp.minimum(neg_z, 0.0) - jnp.log(1.0 + jnp.exp(jnp.minimum(z, neg_z)))
    if mask is not None:
        log_1m = jnp.where(mask, log_1m, 0.0)
    suffix = jnp.dot(log_1m.astype(BF16), tri, preferred_element_type=F32)
    return suffix, jnp.sum(log_1m, axis=-1, keepdims=True)


def _sb_weights(z, suffix, mask):
    w = jnp.exp(z + suffix)
    if mask is not None:
        w = jnp.where(mask, w, 0.0)
    return w.astype(BF16)


def _layer_kernel(x_ref, mod_ref, ng_ref, win_ref, qg_ref, kg_ref, wdw_ref, bdw_ref, lng_ref, lnb_ref,
                  wca_ref, wsb_ref, wout_ref, hsum_ref, tri_ref,
                  o_ref,
                  kt_s, ve_s, vo_s, qe_s, qo_s, gh_s, sh_s, att_s):
    i = pl.program_id(1)
    t = x_ref.shape[1]
    d = D_MODEL

    x = x_ref[0]
    mod = mod_ref[0]
    shift = mod[:, 0:d]
    scale = mod[:, d:2 * d]
    gate = mod[:, 2 * d:3 * d]

    rs = lax.rsqrt(jnp.mean(x * x, axis=-1, keepdims=True) + EPS)
    h = (x * rs * ng_ref[...]) * (1.0 + scale) + shift
    hb = h.astype(BF16)

    def proj(lo, width):
        return jnp.dot(hb, win_ref[:, lo:lo + width], preferred_element_type=F32)

    @pl.when(i == 0)
    def _():
        gh_s[0:CONV_HIST, :] = jnp.zeros((CONV_HIST, D_CONV), F32)

    gh_s[CONV_HIST:CONV_HIST + t, :] = proj(C_GLU_A, D_CONV) * _sigmoid(proj(C_GLU_B, D_CONV))

    lane = lax.broadcasted_iota(jnp.int32, (t, D_SB), 1)
    even = (lane & SB_HEAD_DIM) == 0
    col0 = pl.multiple_of(i * t, t)
    rows_now = pl.ds(col0, t)
    half = D_MODEL // 2

    def head_norm(v, gain):
        ms = jnp.dot((v * v).astype(BF16), hsum_ref[...], preferred_element_type=F32)
        return v * lax.rsqrt(ms + EPS) * gain

    def do_q():
        qn = head_norm(proj(C_Q, D_SB), qg_ref[...]) * (SB_HEAD_DIM ** -0.5)
        qe_s[...] = jnp.where(even, qn, 0.0).astype(BF16)
        qo_s[...] = jnp.where(even, 0.0, qn).astype(BF16)

    def do_k():
        kt_s[:, rows_now] = head_norm(proj(C_K, D_SB), kg_ref[...]).T.astype(BF16)

    def do_v():
        v = proj(C_V, D_SB)
        ve_s[rows_now, :] = jnp.where(even, v, 0.0).astype(BF16)
        vo_s[rows_now, :] = jnp.where(even, 0.0, v).astype(BF16)

    side = {}
    mxu_work = [
        do_q, do_k, do_v,
        lambda: side.update(gate_a=_silu(proj(C_GATE_A, D_CONV))),
        lambda: side.update(gate_b=_silu(proj(C_GATE_B, D_SB))),
    ]
    base = CONV_HIST - (CONV_WIDTH - 1)
    conv = jnp.broadcast_to(bdw_ref[...], (t, D_CONV))
    for r in range(SUBLANES):
        if r < len(mxu_work):
            mxu_work[r]()
        if r > 0:
            sh_s[r - 1] = gh_s[r:r + SHIFT_ROWS, :]
        for k in range(CONV_WIDTH):
            if (base + k) % SUBLANES == r:
                a = (base + k) // SUBLANES * SUBLANES
                rows = gh_s[a:a + t, :] if r == 0 else sh_s[r - 1, a:a + t, :]
                conv = conv + wdw_ref[k:k + 1, :] * rows
    gh_s[0:CONV_HIST, :] = gh_s[t:t + CONV_HIST, :]

    mu = jnp.mean(conv, axis=-1, keepdims=True)
    xc = conv - mu
    var = jnp.mean(xc * xc, axis=-1, keepdims=True)
    ln = xc * lax.rsqrt(var + EPS) * lng_ref[...] + lnb_ref[...]
    ca = (_silu(ln) * side["gate_a"]).astype(BF16)

    merge_gates = [
        lambda: side.update(m_a0=_sigmoid(proj(C_M_A, half))),
        lambda: side.update(m_a1=_sigmoid(proj(C_M_A + half, half))),
        lambda: side.update(m_b0=_sigmoid(proj(C_M_B, half))),
        lambda: side.update(m_b1=_sigmoid(proj(C_M_B + half, half))),
    ]

    lanes = [slice(h // 2 * LANES, (h // 2 + 1) * LANES) for h in range(SB_HEADS)]
    q_s = [qe_s, qo_s]
    v_s = [ve_s, vo_s]

    def blocks_pass(jobs, skew, fillers=()):
        fillers = list(fillers)
        units = [(n, h) for n in range(len(jobs)) for h in range(SB_HEADS)]
        every = max(1, len(units) // max(1, len(fillers)))
        keys = [pl.ds(pl.multiple_of(j * KEY_BLOCK, KEY_BLOCK), KEY_BLOCK) for _, j, _ in jobs]
        z, terms, pv = [], [], []
        for step in range(len(units) + 2 * skew):
            if step < len(units):
                n, h = units[step]
                z.append(jnp.dot(q_s[h % 2][jobs[n][0], lanes[h]], kt_s[lanes[h], keys[n]],
                                 preferred_element_type=F32))
            if skew <= step < len(units) + skew:
                u = step - skew
                if fillers and u % every == 0:
                    fillers.pop(0)()
                terms.append(_sb_scores(z[u], tri_ref[...], jobs[units[u][0]][2]))
            if step >= 2 * skew:
                u = step - 2 * skew
                n, h = units[u]
                pv.append(jnp.dot(_sb_weights(z[u], terms[u][0], jobs[n][2]), v_s[h % 2][keys[n], lanes[h]],
                                  preferred_element_type=F32))
        out = [[pv[n * SB_HEADS + h] for h in range(SB_HEADS)] for n in range(len(jobs))]
        sums = [[terms[n * SB_HEADS + h][1] for h in range(SB_HEADS)] for n in range(len(jobs))]
        return out, sums

    def live(cs):
        top = cs[0]
        for c in cs[1:]:
            top = jnp.maximum(top, c)
        return (jnp.max(jnp.exp(top)) > 0.0).astype(jnp.int32)

    row = lax.broadcasted_iota(jnp.int32, (KEY_BLOCK, KEY_BLOCK), 0)
    colk = lax.broadcasted_iota(jnp.int32, (KEY_BLOCK, KEY_BLOCK), 1)
    causal = colk < row
    subs = t // KEY_BLOCK
    first = i * subs
    sub_rows = [slice(s * KEY_BLOCK, (s + 1) * KEY_BLOCK) for s in range(subs)]

    jobs = []
    for s in range(subs):
        jobs.append((sub_rows[s], first + s, causal))
        jobs += [(sub_rows[s], first + e, None) for e in reversed(range(s))]
    pv, sums = blocks_pass(jobs, TILE_PASS_SKEW, merge_gates)
    carries = []
    n = 0
    for s in range(subs):
        acc, cs = list(pv[n]), list(sums[n])
        for m in range(n + 1, n + 1 + s):
            acc = [acc[h] + pv[m][h] * jnp.exp(cs[h]) for h in range(SB_HEADS)]
            cs = [cs[h] + sums[m][h] for h in range(SB_HEADS)]
        n += 1 + s
        for p in range(HEAD_PAIRS):
            att_s[sub_rows[s], lanes[2 * p]] = acc[2 * p] + acc[2 * p + 1]
        carries.append(tuple(cs))

    for s in range(subs):
        def earlier_block(st, rows=sub_rows[s]):
            (pv_j,), (sums_j,) = blocks_pass([(rows, first - 1 - st[0], None)], LOOP_PASS_SKEW)
            cs = st[2:]
            for p in range(HEAD_PAIRS):
                e, o = 2 * p, 2 * p + 1
                att_s[rows, lanes[e]] += pv_j[e] * jnp.exp(cs[e]) + pv_j[o] * jnp.exp(cs[o])
            cs = tuple(cs[h] + sums_j[h] for h in range(SB_HEADS))
            return (st[0] + 1, live(cs)) + cs

        lax.while_loop(lambda st: jnp.logical_and(st[0] < first, st[1] > 0), earlier_block,
                       (jnp.int32(0), live(carries[s])) + carries[s])

    y_a = jnp.dot(ca, wca_ref[...], preferred_element_type=F32)
    y_a = jnp.concatenate([side["m_a0"] * y_a[:, :half], side["m_a1"] * y_a[:, half:]], axis=1)

    ob = att_s[...] * side["gate_b"]
    y_b = jnp.dot(ob.astype(BF16), wsb_ref[...], preferred_element_type=F32)

    y = y_a + jnp.concatenate([side["m_b0"] * y_b[:, :half], side["m_b1"] * y_b[:, half:]], axis=1)
    o_ref[0] = x + gate * jnp.dot(y.astype(BF16), wout_ref[...], preferred_element_type=F32)


def _resident(shape):
    zeros = (0,) * len(shape)
    return pl.BlockSpec(shape, lambda b, i: zeros, pipeline_mode=pl.Buffered(1))


def _layer(x, mod, ng, win, qg, kg, wdw, bdw, lng, lnb, wca, wsb, wout, hsum, tri):
    bsz, seq, d = x.shape
    t = SEQ_TILE
    n_tiles = seq // t
    consts = (ng, win, qg, kg, wdw, bdw, lng, lnb, wca, wsb, wout, hsum, tri)
    return pl.pallas_call(
        _layer_kernel,
        grid=(bsz, n_tiles),
        in_specs=[
            pl.BlockSpec((1, t, d), lambda b, i: (b, i, 0)),
            pl.BlockSpec((1, 1, 3 * d), lambda b, i: (b, 0, 0)),
        ] + [_resident(a.shape) for a in consts],
        out_specs=pl.BlockSpec((1, t, d), lambda b, i: (b, i, 0)),
        out_shape=jax.ShapeDtypeStruct(x.shape, x.dtype),
        scratch_shapes=[
            pltpu.VMEM((D_SB, seq), BF16),
            pltpu.VMEM((seq, D_SB), BF16),
            pltpu.VMEM((seq, D_SB), BF16),
            pltpu.VMEM((t, D_SB), BF16),
            pltpu.VMEM((t, D_SB), BF16),
            pltpu.VMEM((CONV_HIST + t, D_CONV), F32),
            pltpu.VMEM((SUBLANES - 1, SHIFT_ROWS, D_CONV), F32),
            pltpu.VMEM((t, D_SB), F32),
        ],
        compiler_params=pltpu.CompilerParams(
            dimension_semantics=("arbitrary", "arbitrary"),
            vmem_limit_bytes=VMEM_LIMIT_BYTES),
        name="mixer_layer",
    )(x, mod, *consts)


def kernel(x, c, w_ada, b_ada, norm_g, w_in, q_gain, k_gain, w_dw, b_dw, ln_g, ln_b,
           w_conv_out, w_sb_out, w_out):
    depth = w_in.shape[0]
    bsz = x.shape[0]
    mod = _ada_modulation(c, w_ada, b_ada)

    head = jnp.arange(D_SB, dtype=jnp.int32) // SB_HEAD_DIM
    hsum = jnp.where(head[:, None] == head[None, :], 1.0 / SB_HEAD_DIM, 0.0).astype(BF16)
    idx = jnp.arange(KEY_BLOCK, dtype=jnp.int32)
    tri = (idx[:, None] >= idx[None, :]).astype(BF16)

    for l in range(depth):
        x = _layer(
            x, mod[l].reshape(bsz, 1, 3 * D_MODEL), norm_g[l].reshape(1, D_MODEL),
            w_in[l].astype(BF16),
            jnp.tile(q_gain[l], SB_HEADS).reshape(1, D_SB), jnp.tile(k_gain[l], SB_HEADS).reshape(1, D_SB),
            w_dw[l], b_dw[l].reshape(1, D_CONV), ln_g[l].reshape(1, D_CONV), ln_b[l].reshape(1, D_CONV),
            w_conv_out[l].astype(BF16), w_sb_out[l].astype(BF16), w_out[l].astype(BF16),
            hsum, tri)
    return x
```

```python
import jax
import jax.numpy as jnp
from jax import lax
from jax.experimental import pallas as pl
from jax.experimental.pallas import tpu as pltpu

D_MODEL = 1024
D_CONV = 512
SB_HEADS = 8
SB_HEAD_DIM = 64
D_SB = SB_HEADS * SB_HEAD_DIM
CONV_WIDTH = 31
EPS = 1e-6

C_GLU_A = 0
C_GLU_B = C_GLU_A + D_CONV
C_GATE_A = C_GLU_B + D_CONV
C_Q = C_GATE_A + D_CONV
C_K = C_Q + D_SB
C_V = C_K + D_SB
C_GATE_B = C_V + D_SB
C_M_A = C_GATE_B + D_SB
C_M_B = C_M_A + D_MODEL
D_IN = C_M_B + D_MODEL

LANES = 128
SUBLANES = 8
HEAD_PAIRS = D_SB // LANES
SEQ_TILE = 512
KEY_BLOCK = 256
TILE_PASS_SKEW = 2
LOOP_PASS_SKEW = 1
CONV_HIST = 32
SHIFT_ROWS = SEQ_TILE + CONV_HIST - SUBLANES
VMEM_LIMIT_BYTES = 58 * 1024 * 1024

F32 = jnp.float32
BF16 = jnp.bfloat16


def _sigmoid(v):
    return jax.nn.sigmoid(v)


def _silu(v):
    return v * jax.nn.sigmoid(v)


def _ada_kernel(c_ref, w_ref, b_ref, o_ref):
    c = c_ref[...]
    o_ref[0] = jnp.dot(_silu(c), w_ref[0], precision=lax.Precision.HIGHEST,
                       preferred_element_type=F32) + b_ref[0]


def _ada_modulation(c, w_ada, b_ada):
    depth, d, d3 = w_ada.shape
    bsz = c.shape[0]
    n_col = d3 // d
    return pl.pallas_call(
        _ada_kernel,
        grid=(depth, n_col),
        in_specs=[
            pl.BlockSpec((bsz, d), lambda l, j: (0, 0)),
            pl.BlockSpec((1, d, d), lambda l, j: (l, 0, j)),
            pl.BlockSpec((1, 1, d), lambda l, j: (l, 0, j)),
        ],
        out_specs=pl.BlockSpec((1, bsz, d), lambda l, j: (l, 0, j)),
        out_shape=jax.ShapeDtypeStruct((depth, bsz, d3), F32),
        name="ada_modulation",
    )(c, w_ada, b_ada.reshape(depth, 1, d3))


def _sb_scores(z, tri, mask):
    neg_z = -z
    log_1m = jnp.minimum(neg_z, 0.0) - jnp.log(1.0 + jnp.exp(jnp.minimum(z, neg_z)))
    if mask is not None:
        log_1m = jnp.where(mask, log_1m, 0.0)
    suffix = jnp.dot(log_1m.astype(BF16), tri, preferred_element_type=F32)
    return suffix, jnp.sum(log_1m, axis=-1, keepdims=True)


def _sb_weights(z, suffix, mask):
    w = jnp.exp(z + suffix)
    if mask is not None:
        w = jnp.where(mask, w, 0.0)
    return w.astype(BF16)


def _layer_kernel(x_ref, mod_ref, ng_ref, win_ref, qg_ref, kg_ref, wdw_ref, bdw_ref, lng_ref, lnb_ref,
                  wca_ref, wsb_ref, wout_ref, hsum_ref, tri_ref,
                  o_ref,
                  kt_s, ve_s, vo_s, qe_s, qo_s, gh_s, sh_s, att_s):
    i = pl.program_id(1)
    t = x_ref.shape[1]
    d = D_MODEL

    x = x_ref[0]
    mod = mod_ref[0]
    shift = mod[:, 0:d]
    scale = mod[:, d:2 * d]
    gate = mod[:, 2 * d:3 * d]

    rs = lax.rsqrt(jnp.mean(x * x, axis=-1, keepdims=True) + EPS)
    h = (x * rs * ng_ref[...]) * (1.0 + scale) + shift
    hb = h.astype(BF16)

    def proj(lo, width):
        return jnp.dot(hb, win_ref[:, lo:lo + width], preferred_element_type=F32)

    @pl.when(i == 0)
    def _():
        gh_s[0:CONV_HIST, :] = jnp.zeros((CONV_HIST, D_CONV), F32)

    gh_s[CONV_HIST:CONV_HIST + t, :] = proj(C_GLU_A, D_CONV) * _sigmoid(proj(C_GLU_B, D_CONV))

    lane = lax.broadcasted_iota(jnp.int32, (t, D_SB), 1)
    even = (lane & SB_HEAD_DIM) == 0
    col0 = pl.multiple_of(i * t, t)
    rows_now = pl.ds(col0, t)
    half = D_MODEL // 2

    def head_norm(v, gain):
        ms = jnp.dot((v * v).astype(BF16), hsum_ref[...], preferred_element_type=F32)
        return v * lax.rsqrt(ms + EPS) * gain

    def do_q():
        qn = head_norm(proj(C_Q, D_SB), qg_ref[...]) * (SB_HEAD_DIM ** -0.5)
        qe_s[...] = jnp.where(even, qn, 0.0).astype(BF16)
        qo_s[...] = jnp.where(even, 0.0, qn).astype(BF16)

    def do_k():
        kt_s[:, rows_now] = head_norm(proj(C_K, D_SB), kg_ref[...]).T.astype(BF16)

    def do_v():
        v = proj(C_V, D_SB)
        ve_s[rows_now, :] = jnp.where(even, v, 0.0).astype(BF16)
        vo_s[rows_now, :] = jnp.where(even, 0.0, v).astype(BF16)

    side = {}
    mxu_work = [
        do_q, do_k, do_v,
        lambda: side.update(gate_a=_silu(proj(C_GATE_A, D_CONV))),
        lambda: side.update(gate_b=_silu(proj(C_GATE_B, D_SB))),
    ]
    base = CONV_HIST - (CONV_WIDTH - 1)
    conv = jnp.broadcast_to(bdw_ref[...], (t, D_CONV))
    for r in range(SUBLANES):
        if r < len(mxu_work):
            mxu_work[r]()
        if r > 0:
            sh_s[r - 1] = gh_s[r:r + SHIFT_ROWS, :]
        for k in range(CONV_WIDTH):
            if (base + k) % SUBLANES == r:
                a = (base + k) // SUBLANES * SUBLANES
                rows = gh_s[a:a + t, :] if r == 0 else sh_s[r - 1, a:a + t, :]
                conv = conv + wdw_ref[k:k + 1, :] * rows
    gh_s[0:CONV_HIST, :] = gh_s[t:t + CONV_HIST, :]

    mu = jnp.mean(conv, axis=-1, keepdims=True)
    xc = conv - mu
    var = jnp.mean(xc * xc, axis=-1, keepdims=True)
    ln = xc * lax.rsqrt(var + EPS) * lng_ref[...] + lnb_ref[...]
    ca = (_silu(ln) * side["gate_a"]).astype(BF16)

    quarter = half // 2
    merge_gates = [lambda n=n: side.update({"mg%d" % n: _sigmoid(proj(C_M_A + n * quarter, quarter))})
                   for n in range(2 * D_MODEL // quarter)]

    lanes = [slice(h // 2 * LANES, (h // 2 + 1) * LANES) for h in range(SB_HEADS)]
    q_s = [qe_s, qo_s]
    v_s = [ve_s, vo_s]

    def blocks_pass(jobs, skew, fillers=()):
        fillers = list(fillers)
        units = [(n, h) for n in range(len(jobs)) for h in range(SB_HEADS)]
        every = max(1, len(units) // max(1, len(fillers)))
        keys = [pl.ds(pl.multiple_of(j * KEY_BLOCK, KEY_BLOCK), KEY_BLOCK) for _, j, _ in jobs]
        z, terms, pv = [], [], []
        for step in range(len(units) + 2 * skew):
            if step < len(units):
                n, h = units[step]
                z.append(jnp.dot(q_s[h % 2][jobs[n][0], lanes[h]], kt_s[lanes[h], keys[n]],
                                 preferred_element_type=F32))
            if skew <= step < len(units) + skew:
                u = step - skew
                if fillers and u % every == 0:
                    fillers.pop(0)()
                terms.append(_sb_scores(z[u], tri_ref[...], jobs[units[u][0]][2]))
            if step >= 2 * skew:
                u = step - 2 * skew
                n, h = units[u]
                pv.append(jnp.dot(_sb_weights(z[u], terms[u][0], jobs[n][2]), v_s[h % 2][keys[n], lanes[h]],
                                  preferred_element_type=F32))
        out = [[pv[n * SB_HEADS + h] for h in range(SB_HEADS)] for n in range(len(jobs))]
        sums = [[terms[n * SB_HEADS + h][1] for h in range(SB_HEADS)] for n in range(len(jobs))]
        return out, sums

    def live(cs):
        top = cs[0]
        for c in cs[1:]:
            top = jnp.maximum(top, c)
        return (jnp.max(jnp.exp(top)) > 0.0).astype(jnp.int32)

    row = lax.broadcasted_iota(jnp.int32, (KEY_BLOCK, KEY_BLOCK), 0)
    colk = lax.broadcasted_iota(jnp.int32, (KEY_BLOCK, KEY_BLOCK), 1)
    causal = colk < row
    subs = t // KEY_BLOCK
    first = i * subs
    sub_rows = [slice(s * KEY_BLOCK, (s + 1) * KEY_BLOCK) for s in range(subs)]

    jobs = []
    for s in range(subs):
        jobs.append((sub_rows[s], first + s, causal))
        jobs += [(sub_rows[s], first + e, None) for e in reversed(range(s))]
    pv, sums = blocks_pass(jobs, TILE_PASS_SKEW, merge_gates)
    carries = []
    n = 0
    for s in range(subs):
        acc, cs = list(pv[n]), list(sums[n])
        for m in range(n + 1, n + 1 + s):
            acc = [acc[h] + pv[m][h] * jnp.exp(cs[h]) for h in range(SB_HEADS)]
            cs = [cs[h] + sums[m][h] for h in range(SB_HEADS)]
        n += 1 + s
        for p in range(HEAD_PAIRS):
            att_s[sub_rows[s], lanes[2 * p]] = acc[2 * p] + acc[2 * p + 1]
        carries.append(tuple(cs))

    for s in range(subs):
        def earlier_block(st, rows=sub_rows[s]):
            (pv_j,), (sums_j,) = blocks_pass([(rows, first - 1 - st[0], None)], LOOP_PASS_SKEW)
            cs = st[2:]
            for p in range(HEAD_PAIRS):
                e, o = 2 * p, 2 * p + 1
                att_s[rows, lanes[e]] += pv_j[e] * jnp.exp(cs[e]) + pv_j[o] * jnp.exp(cs[o])
            cs = tuple(cs[h] + sums_j[h] for h in range(SB_HEADS))
            return (st[0] + 1, live(cs)) + cs

        lax.while_loop(lambda st: jnp.logical_and(st[0] < first, st[1] > 0), earlier_block,
                       (jnp.int32(0), live(carries[s])) + carries[s])

    y_a = jnp.dot(ca, wca_ref[...], preferred_element_type=F32)
    y_a = jnp.concatenate([side["mg%d" % n] * y_a[:, n * quarter:(n + 1) * quarter] for n in range(4)], axis=1)

    ob = att_s[...] * side["gate_b"]
    y_b = jnp.dot(ob.astype(BF16), wsb_ref[...], preferred_element_type=F32)

    y = y_a + jnp.concatenate([side["mg%d" % (4 + n)] * y_b[:, n * quarter:(n + 1) * quarter] for n in range(4)], axis=1)
    o_ref[0] = x + gate * jnp.dot(y.astype(BF16), wout_ref[...], preferred_element_type=F32)


def _resident(shape):
    zeros = (0,) * len(shape)
    return pl.BlockSpec(shape, lambda b, i: zeros, pipeline_mode=pl.Buffered(1))


def _layer(x, mod, ng, win, qg, kg, wdw, bdw, lng, lnb, wca, wsb, wout, hsum, tri):
    bsz, seq, d = x.shape
    t = SEQ_TILE
    n_tiles = seq // t
    consts = (ng, win, qg, kg, wdw, bdw, lng, lnb, wca, wsb, wout, hsum, tri)
    return pl.pallas_call(
        _layer_kernel,
        grid=(bsz, n_tiles),
        in_specs=[
            pl.BlockSpec((1, t, d), lambda b, i: (b, i, 0)),
            pl.BlockSpec((1, 1, 3 * d), lambda b, i: (b, 0, 0)),
        ] + [_resident(a.shape) for a in consts],
        out_specs=pl.BlockSpec((1, t, d), lambda b, i: (b, i, 0)),
        out_shape=jax.ShapeDtypeStruct(x.shape, x.dtype),
        scratch_shapes=[
            pltpu.VMEM((D_SB, seq), BF16),
            pltpu.VMEM((seq, D_SB), BF16),
            pltpu.VMEM((seq, D_SB), BF16),
            pltpu.VMEM((t, D_SB), BF16),
            pltpu.VMEM((t, D_SB), BF16),
            pltpu.VMEM((CONV_HIST + t, D_CONV), F32),
            pltpu.VMEM((SUBLANES - 1, SHIFT_ROWS, D_CONV), F32),
            pltpu.VMEM((t, D_SB), F32),
        ],
        compiler_params=pltpu.CompilerParams(
            dimension_semantics=("arbitrary", "arbitrary"),
            vmem_limit_bytes=VMEM_LIMIT_BYTES),
        name="mixer_layer",
    )(x, mod, *consts)


def kernel(x, c, w_ada, b_ada, norm_g, w_in, q_gain, k_gain, w_dw, b_dw, ln_g, ln_b,
           w_conv_out, w_sb_out, w_out):
    depth = w_in.shape[0]
    bsz = x.shape[0]
    mod = _ada_modulation(c, w_ada, b_ada)

    head = jnp.arange(D_SB, dtype=jnp.int32) // SB_HEAD_DIM
    hsum = jnp.where(head[:, None] == head[None, :], 1.0 / SB_HEAD_DIM, 0.0).astype(BF16)
    idx = jnp.arange(KEY_BLOCK, dtype=jnp.int32)
    tri = (idx[:, None] >= idx[None, :]).astype(BF16)

    for l in range(depth):
        x = _layer(
            x, mod[l].reshape(bsz, 1, 3 * D_MODEL), norm_g[l].reshape(1, D_MODEL),
            w_in[l].astype(BF16),
            jnp.tile(q_gain[l], SB_HEADS).reshape(1, D_SB), jnp.tile(k_gain[l], SB_HEADS).reshape(1, D_SB),
            w_dw[l], b_dw[l].reshape(1, D_CONV), ln_g[l].reshape(1, D_CONV), ln_b[l].reshape(1, D_CONV),
            w_conv_out[l].astype(BF16), w_sb_out[l].astype(BF16), w_out[l].astype(BF16),
            hsum, tri)
    return x
```
